```python
import math
import jax, jax.numpy as jnp
from jax import lax
import numpy as np

D_MODEL = 1024
BATCH = 16
SEQ = 4096
DEPTH = 1

A_HEADS = 8
A_KV_HEADS = 2
A_HEAD_DIM = 64
A_WIDTH = A_HEADS * A_HEAD_DIM
A_KV_WIDTH = A_KV_HEADS * A_HEAD_DIM
WINDOW = 128
BLOCK = 128
ROPE_THETA = 500000.0
ROPE_DIM = A_HEAD_DIM // 4
B_HEADS = 4
B_KEY_DIM = 64
B_VAL_DIM = 128
B_KEY_WIDTH = B_HEADS * B_KEY_DIM
B_WIDTH = B_HEADS * B_VAL_DIM
B_GATE_RANK = 16
B_GATE_TEMP = 16.0
B_CHUNK = 64
NORM_EPS = 1e-6
NEG_BIG = -1e30

IN_SPLITS = (A_WIDTH, A_KV_WIDTH, A_KV_WIDTH, A_WIDTH,
             B_KEY_WIDTH, B_KEY_WIDTH, B_WIDTH, B_WIDTH, B_GATE_RANK,
             D_MODEL, D_MODEL)
D_IN = 2 * A_WIDTH + 2 * A_KV_WIDTH + 2 * B_KEY_WIDTH + 2 * B_WIDTH + B_GATE_RANK + 2 * D_MODEL

kernel_name = "hybrid_swa_sink_gla_gated_merge"


def rms_norm(x, g):
    xf = x.astype(jnp.float32)
    y = xf * lax.rsqrt(jnp.mean(xf * xf, axis=-1, keepdims=True) + NORM_EPS)
    return (y * g.astype(jnp.float32)).astype(x.dtype)


def partial_rope(t, positions):
    half = ROPE_DIM // 2
    inv_freq = jnp.exp(-math.log(ROPE_THETA) * jnp.arange(half, dtype=jnp.float32) * (2.0 / ROPE_DIM))
    ang = positions.astype(jnp.float32)[..., None] * inv_freq
    cos = jnp.cos(ang)[:, :, None, :]
    sin = jnp.sin(ang)[:, :, None, :]
    tr = t[..., :ROPE_DIM].astype(jnp.float32)
    t1, t2 = tr[..., :half], tr[..., half:]
    rot = jnp.concatenate([t1 * cos - t2 * sin, t2 * cos + t1 * sin], axis=-1)
    return jnp.concatenate([rot.astype(t.dtype), t[..., ROPE_DIM:]], axis=-1)


def sliding_window_gqa_sinks(q, k, v, sinks):
    b, s = q.shape[0], q.shape[1]
    nb = s // BLOCK
    grp = A_HEADS // A_KV_HEADS
    qb = q.reshape(b, nb, BLOCK, A_KV_HEADS, grp, A_HEAD_DIM).astype(jnp.float32)
    kb = k.reshape(b, nb, BLOCK, A_KV_HEADS, A_HEAD_DIM).astype(jnp.float32)
    vb = v.reshape(b, nb, BLOCK, A_KV_HEADS, A_HEAD_DIM).astype(jnp.float32)

    def with_prev(t):
        prev = jnp.pad(t[:, :-1], ((0, 0), (1, 0), (0, 0), (0, 0), (0, 0)))
        return jnp.concatenate([prev, t], axis=2)

    kw, vw = with_prev(kb), with_prev(vb)
    scores = jnp.einsum('bnqhgd,bnkhd->bnhgqk', qb, kw) * (A_HEAD_DIM ** -0.5)
    qi = jnp.arange(BLOCK)[:, None]
    kj = jnp.arange(2 * BLOCK)[None, :] - BLOCK
    band = (kj <= qi) & (qi - kj < WINDOW)
    blk = jnp.arange(nb)[:, None, None]
    valid = band[None] & ((blk > 0) | (kj[None] >= 0))
    scores = jnp.where(valid[None, :, None, None], scores, NEG_BIG)
    sink = sinks.astype(jnp.float32).reshape(A_KV_HEADS, grp)[None, None, :, :, None, None]
    m = jnp.maximum(jnp.max(scores, axis=-1, keepdims=True), sink)
    p = jnp.exp(scores - m)
    denom = jnp.sum(p, axis=-1, keepdims=True) + jnp.exp(sink - m)
    probs = p / denom
    out = jnp.einsum('bnhgqk,bnkhd->bnqhgd', probs, vw)
    return out.reshape(b, s, A_WIDTH).astype(q.dtype)


def gla_chunked(q, k, v, log_a):
    b, s = q.shape[0], q.shape[1]
    nc = s // B_CHUNK

    def chunks(t):
        return t.astype(jnp.float32).reshape(b, nc, B_CHUNK, B_HEADS, t.shape[-1]).transpose(0, 3, 1, 2, 4)

    qc = chunks(q) * (B_KEY_DIM ** -0.5)
    kc, vc, gc = chunks(k), chunks(v), chunks(log_a)
    cum = jnp.cumsum(gc, axis=3)
    last = cum[:, :, :, -1:, :]
    mid = cum[:, :, :, B_CHUNK // 2 - 1:B_CHUNK // 2, :]
    attn = jnp.einsum('bhnid,bhnjd->bhnij', qc * jnp.exp(cum - mid), kc * jnp.exp(mid - cum))
    causal = jnp.tril(jnp.ones((B_CHUNK, B_CHUNK), dtype=bool))
    attn = jnp.where(causal, attn, 0.0)
    o_intra = jnp.einsum('bhnij,bhnjv->bhniv', attn, vc)
    inc = jnp.einsum('bhnjd,bhnjv->nbhdv', kc * jnp.exp(last - cum), vc)
    decay = jnp.exp(last[:, :, :, 0, :]).transpose(2, 0, 1, 3)

    def step(state, xs):
        dec, add = xs
        return dec[..., None] * state + add, state

    init = jnp.zeros((b, B_HEADS, B_KEY_DIM, B_VAL_DIM), jnp.float32)
    _, s_prev = lax.scan(step, init, (decay, inc))
    o_inter = jnp.einsum('bhnid,nbhdv->bhniv', qc * jnp.exp(cum), s_prev)
    o = (o_intra + o_inter).transpose(0, 2, 3, 1, 4)
    return o.reshape(b, s, B_HEADS, B_VAL_DIM)


def hybrid_layer(x, positions, g_in, w_in, w_alpha_up, b_alpha, attn_sinks, g_gla_norm,
                 w_out_a, w_out_b, w_o):
    b, s = x.shape[0], x.shape[1]
    h = rms_norm(x, g_in)
    proj = jnp.einsum('bsd,de->bse', h, w_in)
    split_points = np.cumsum(IN_SPLITS)[:-1].tolist()
    qa, ka, va, za, qb, kb, vb, zb, a_lr, gate_a, gate_b = jnp.split(proj, split_points, axis=-1)

    qa = partial_rope(qa.reshape(b, s, A_HEADS, A_HEAD_DIM), positions)
    ka = partial_rope(ka.reshape(b, s, A_KV_HEADS, A_HEAD_DIM), positions)
    va = va.reshape(b, s, A_KV_HEADS, A_HEAD_DIM)
    oa = sliding_window_gqa_sinks(qa, ka, va, attn_sinks) * jax.nn.silu(za)
    ya = jnp.einsum('bse,ed->bsd', oa, w_out_a)

    log_a = jax.nn.log_sigmoid(
        (jnp.einsum('bsr,re->bse', a_lr, w_alpha_up) + b_alpha).astype(jnp.float32)) / B_GATE_TEMP
    ob = gla_chunked(qb.reshape(b, s, B_HEADS, B_KEY_DIM),
                     kb.reshape(b, s, B_HEADS, B_KEY_DIM),
                     vb.reshape(b, s, B_HEADS, B_VAL_DIM),
                     log_a.reshape(b, s, B_HEADS, B_KEY_DIM))
    ob = ob * lax.rsqrt(jnp.mean(ob * ob, axis=-1, keepdims=True) + NORM_EPS)
    ob = (ob.reshape(b, s, B_WIDTH) * g_gla_norm.astype(jnp.float32)).astype(x.dtype) * jax.nn.silu(zb)
    yb = jnp.einsum('bse,ed->bsd', ob, w_out_b)

    merged = jax.nn.sigmoid(gate_a) * ya + jax.nn.sigmoid(gate_b) * yb
    return x + jnp.einsum('bsd,de->bse', merged, w_o)


def setup_inputs(seed: int = 0) -> dict:
    key = jax.random.key(seed)
    ks = jax.random.split(key, 12)
    nrm = jax.random.normal
    f32 = jnp.float32
    x = nrm(ks[0], (BATCH, SEQ, D_MODEL), f32)
    positions = jnp.broadcast_to(jnp.arange(SEQ, dtype=jnp.int32)[None, :], (BATCH, SEQ))
    g_in = 1.0 + 0.02 * nrm(ks[1], (DEPTH, D_MODEL), f32)
    w_in = nrm(ks[2], (DEPTH, D_MODEL, D_IN), f32) * D_MODEL ** -0.5
    w_alpha_up = nrm(ks[3], (DEPTH, B_GATE_RANK, B_KEY_WIDTH), f32) * B_GATE_RANK ** -0.5
    b_alpha = 0.1 * nrm(ks[4], (DEPTH, B_KEY_WIDTH), f32)
    attn_sinks = 0.5 * nrm(ks[5], (DEPTH, A_HEADS), f32)
    g_gla_norm = 1.0 + 0.02 * nrm(ks[6], (DEPTH, B_WIDTH), f32)
    w_out_a = nrm(ks[7], (DEPTH, A_WIDTH, D_MODEL), f32) * A_WIDTH ** -0.5
    w_out_b = nrm(ks[8], (DEPTH, B_WIDTH, D_MODEL), f32) * B_WIDTH ** -0.5
    w_o = nrm(ks[9], (DEPTH, D_MODEL, D_MODEL), f32) * D_MODEL ** -0.5
    g_final = 1.0 + 0.02 * nrm(ks[10], (D_MODEL,), f32)
    return {"x": x, "positions": positions, "g_in": g_in, "w_in": w_in,
            "w_alpha_up": w_alpha_up, "b_alpha": b_alpha, "attn_sinks": attn_sinks,
            "g_gla_norm": g_gla_norm, "w_out_a": w_out_a, "w_out_b": w_out_b,
            "w_o": w_o, "g_final": g_final}


def reference(x, positions, g_in, w_in, w_alpha_up, b_alpha, attn_sinks, g_gla_norm,
              w_out_a, w_out_b, w_o, g_final):
    h = x
    for layer in range(DEPTH):
        h = hybrid_layer(h, positions, g_in[layer], w_in[layer], w_alpha_up[layer], b_alpha[layer],
                         attn_sinks[layer], g_gla_norm[layer], w_out_a[layer], w_out_b[layer],
                         w_o[layer])
    return rms_norm(h, g_final)
```

```python
import math

import jax
import jax.numpy as jnp
from jax import lax
from jax.experimental import pallas as pl
from jax.experimental.pallas import tpu as pltpu

D_MODEL = 1024
A_HEADS = 8
A_KV_HEADS = 2
A_HEAD_DIM = 64
A_WIDTH = A_HEADS * A_HEAD_DIM
A_KV_WIDTH = A_KV_HEADS * A_HEAD_DIM
WINDOW = 128
ROPE_THETA = 500000.0
ROPE_DIM = A_HEAD_DIM // 4
ROPE_HALF = ROPE_DIM // 2
B_HEADS = 4
B_KEY_DIM = 64
B_VAL_DIM = 128
B_KEY_WIDTH = B_HEADS * B_KEY_DIM
B_WIDTH = B_HEADS * B_VAL_DIM
B_GATE_RANK = 16
B_GATE_TEMP = 16.0
B_CHUNK = 64
NORM_EPS = 1e-6
NEG_BIG = -1e30

LANES = 128
VMEM_LIMIT_BYTES = 56 * 1024 * 1024

SEQ_TILE = 256
MXU_DTYPE = jnp.bfloat16

_COL_A = 0
_COL_B = _COL_A + 2 * A_WIDTH + 2 * A_KV_WIDTH
_ALR_PAD = LANES
_COL_G = _COL_B + 2 * B_KEY_WIDTH + 2 * B_WIDTH + _ALR_PAD
_COL_END = _COL_G + 2 * D_MODEL

_NT = (((1,), (1,)), ((), ()))
_TN = (((0,), (0,)), ((), ()))


def _dot(a, b, dims=None):
    a = a.astype(MXU_DTYPE)
    b = b.astype(MXU_DTYPE)
    if dims is None:
        return jnp.dot(a, b, preferred_element_type=jnp.float32)
    return lax.dot_general(a, b, dims, preferred_element_type=jnp.float32)


def _sigmoid(z):
    return 1.0 / (1.0 + jnp.exp(-z))


def _silu(z):
    return z * _sigmoid(z)


def _rope_tables(pos_row):
    t = pos_row.shape[1]
    fidx = lax.broadcasted_iota(jnp.int32, (ROPE_HALF, 1), 0).astype(jnp.float32)
    inv_freq = jnp.exp(-math.log(ROPE_THETA) * fidx * (2.0 / ROPE_DIM))
    ang = pos_row.astype(jnp.float32) * inv_freq
    stacked = jnp.concatenate(
        [jnp.cos(ang), jnp.sin(ang), jnp.zeros((LANES - 2 * ROPE_HALF, t), jnp.float32)], axis=0)
    tt = stacked.T
    lane = lax.broadcasted_iota(jnp.int32, (t, LANES), 1)
    lo = lane < A_HEAD_DIM
    l64 = lane & (A_HEAD_DIM - 1)
    first = l64 < ROPE_HALF
    second = jnp.logical_and(l64 >= ROPE_HALF, l64 < ROPE_DIM)
    r8 = pltpu.roll(tt, ROPE_HALF, 1)
    r64 = pltpu.roll(tt, A_HEAD_DIM, 1)
    r72 = pltpu.roll(tt, A_HEAD_DIM + ROPE_HALF, 1)
    r120 = pltpu.roll(tt, LANES - ROPE_HALF, 1)
    r56 = pltpu.roll(tt, A_HEAD_DIM - ROPE_HALF, 1)
    cos_t = jnp.where(first, jnp.where(lo, tt, r64), jnp.where(second, jnp.where(lo, r8, r72), 1.0))
    sin_a = jnp.where(first, -jnp.where(lo, r120, r56), 0.0)
    sin_b = jnp.where(second, jnp.where(lo, tt, r64), 0.0)
    return cos_t, sin_a, sin_b


def _rope(xg, tables):
    cos_t, sin_a, sin_b = tables
    return (xg * cos_t + pltpu.roll(xg, LANES - ROPE_HALF, 1) * sin_a
            + pltpu.roll(xg, ROPE_HALF, 1) * sin_b)


def _dup_halves(x):
    lane = lax.broadcasted_iota(jnp.int32, x.shape, 1)
    lo = lane < A_HEAD_DIM
    r = pltpu.roll(x, A_HEAD_DIM, 1)
    return jnp.where(lo, x, r), jnp.where(lo, r, x)


def _attention(q_rot, k_all, v_all, sinks_ref, first_key0):
    t = q_rot.shape[0]
    nblk = t // WINDOW
    grp = A_HEADS // A_KV_HEADS
    rows = grp * WINDOW
    kd = _dup_halves(k_all)
    vd = _dup_halves(v_all)
    kd = [a.astype(MXU_DTYPE) for a in kd]
    vd = [a.astype(MXU_DTYPE) for a in vd]

    lane = lax.broadcasted_iota(jnp.int32, (WINDOW, LANES), 1)
    lo = lane < A_HEAD_DIM
    r_i = lax.broadcasted_iota(jnp.int32, (rows, 2 * WINDOW), 0) & (WINDOW - 1)
    k_j = lax.broadcasted_iota(jnp.int32, (rows, 2 * WINDOW), 1) - WINDOW
    band = jnp.logical_and(k_j <= r_i, r_i - k_j < WINDOW)
    row1 = lax.broadcasted_iota(jnp.int32, (rows, 1), 0)

    out_blocks = []
    for i in range(nblk):
        qs = q_rot[i * WINDOW:(i + 1) * WINDOW, :]
        valid = jnp.logical_and(band, k_j >= first_key0) if i == 0 else band
        pair_out = []
        for g in range(A_KV_HEADS):
            parts = []
            for p in range(grp // 2):
                qp = qs[:, (g * (grp // 2) + p) * LANES:(g * (grp // 2) + p + 1) * LANES]
                parts.append(jnp.where(lo, qp, 0.0))
                parts.append(jnp.where(lo, 0.0, qp))
            q4 = jnp.concatenate(parts, axis=0)
            kb = kd[g][i * WINDOW:i * WINDOW + 2 * WINDOW, :]
            vb = vd[g][i * WINDOW:i * WINDOW + 2 * WINDOW, :]
            s = _dot(q4, kb, _NT)
            s = jnp.where(valid, s, NEG_BIG)
            sink = sinks_ref[g * grp + grp - 1]
            sink_col = jnp.full((rows, 1), sink, jnp.float32)
            for j in range(grp - 2, -1, -1):
                sink_col = jnp.where(row1 < (j + 1) * WINDOW, sinks_ref[g * grp + j], sink_col)
            m = jnp.maximum(jnp.max(s, axis=-1, keepdims=True), sink_col)
            p_ = jnp.exp(s - m)
            denom = jnp.sum(p_, axis=-1, keepdims=True) + jnp.exp(sink_col - m)
            o4 = _dot(p_, vb) / denom
            for p in range(grp // 2):
                pair_out.append(jnp.where(lo, o4[(2 * p) * WINDOW:(2 * p + 1) * WINDOW, :],
                                          o4[(2 * p + 1) * WINDOW:(2 * p + 2) * WINDOW, :]))
        out_blocks.append(jnp.concatenate(pair_out, axis=1))
    return jnp.concatenate(out_blocks, axis=0)


def _chunk_cumsum(g):
    row = lax.broadcasted_iota(jnp.int32, g.shape, 0) & (B_CHUNK - 1)
    k = 1
    while k < B_CHUNK:
        g = g + jnp.where(row >= k, pltpu.roll(g, k, 0), 0.0)
        k *= 2
    return g


def _gla(qb, kb, vb, log_a, st_ref):
    t = qb.shape[0]
    nchunk = t // B_CHUNK
    cum_all = _chunk_cumsum(log_a)
    lane = lax.broadcasted_iota(jnp.int32, (B_CHUNK, LANES), 1)
    lo = lane < B_KEY_DIM
    ri = lax.broadcasted_iota(jnp.int32, (2 * B_CHUNK, B_CHUNK), 0) & (B_CHUNK - 1)
    ci = lax.broadcasted_iota(jnp.int32, (2 * B_CHUNK, B_CHUNK), 1)
    causal = ci <= ri
    scale = B_KEY_DIM ** -0.5

    states = [st_ref[h] for h in range(B_HEADS)]
    out_chunks = []
    for c in range(nchunk):
        r0, r1 = c * B_CHUNK, (c + 1) * B_CHUNK
        head_out = [None] * B_HEADS
        for p in range(B_HEADS // 2):
            l0, l1 = p * LANES, (p + 1) * LANES
            cum = cum_all[r0:r1, l0:l1]
            last = cum[B_CHUNK - 1:B_CHUNK, :]
            mid = cum[B_CHUNK // 2 - 1:B_CHUNK // 2, :]
            qs = qb[r0:r1, l0:l1] * scale
            ks = kb[r0:r1, l0:l1]
            qe = qs * jnp.exp(cum - mid)
            ke = ks * jnp.exp(mid - cum)
            qg = qs * jnp.exp(cum)
            kl = ks * jnp.exp(last - cum)
            decay = jnp.exp(last)
            qe2 = jnp.concatenate([jnp.where(lo, qe, 0.0), jnp.where(lo, 0.0, qe)], axis=0)
            a2 = _dot(qe2, ke, _NT)
            a2 = jnp.where(causal, a2, 0.0)
            for half in range(2):
                h = 2 * p + half
                sel = lo if half == 0 else jnp.logical_not(lo)
                vh = vb[r0:r1, h * B_VAL_DIM:(h + 1) * B_VAL_DIM]
                attn = a2[half * B_CHUNK:(half + 1) * B_CHUNK, :]
                o_intra = _dot(attn, vh)
                o_inter = _dot(jnp.where(sel, qg, 0.0), states[h], _NT)
                head_out[h] = o_intra + o_inter
                inc = _dot(vh, jnp.where(sel, kl, 0.0), _TN)
                states[h] = states[h] * decay + inc
        out_chunks.append(jnp.concatenate(head_out, axis=1))
    for h in range(B_HEADS):
        st_ref[h] = states[h]
    return jnp.concatenate(out_chunks, axis=0)


def _body(sinks_ref, x_ref, pos_ref, gin_ref, wcat_ref, walpha_ref, balpha_ref, ggla_ref,
          woa_ref, wob_ref, wo_ref, gfin_ref, out_ref, kprev_ref, vprev_ref, st_ref):
    s_idx = pl.program_id(1)

    @pl.when(s_idx == 0)
    def _():
        kprev_ref[...] = jnp.zeros_like(kprev_ref)
        vprev_ref[...] = jnp.zeros_like(vprev_ref)
        st_ref[...] = jnp.zeros_like(st_ref)

    x = x_ref[...]
    t = x.shape[0]
    h = x * lax.rsqrt(jnp.mean(x * x, axis=-1, keepdims=True) + NORM_EPS) * gin_ref[...]
    hb = h.astype(MXU_DTYPE)

    pa = _dot(hb, wcat_ref[:, _COL_A:_COL_B])
    tables = _rope_tables(pos_ref[...])
    qscale = A_HEAD_DIM ** -0.5
    q_rot = jnp.concatenate(
        [_rope(pa[:, j * LANES:(j + 1) * LANES], tables) * qscale for j in range(A_WIDTH // LANES)],
        axis=1)
    k_rot = _rope(pa[:, A_WIDTH:A_WIDTH + A_KV_WIDTH], tables)
    va = pa[:, A_WIDTH + A_KV_WIDTH:A_WIDTH + 2 * A_KV_WIDTH]
    za = pa[:, A_WIDTH + 2 * A_KV_WIDTH:2 * A_WIDTH + 2 * A_KV_WIDTH]
    k_all = jnp.concatenate([kprev_ref[...], k_rot], axis=0)
    v_all = jnp.concatenate([vprev_ref[...], va], axis=0)
    kprev_ref[...] = k_rot[t - WINDOW:, :]
    vprev_ref[...] = va[t - WINDOW:, :]
    first_key0 = jnp.where(s_idx > 0, -WINDOW, 0)
    oa = _attention(q_rot, k_all, v_all, sinks_ref, first_key0) * _silu(za)
    ya = _dot(oa, woa_ref[...])

    pb = _dot(hb, wcat_ref[:, _COL_B:_COL_G])
    qb = pb[:, 0:B_KEY_WIDTH]
    kb = pb[:, B_KEY_WIDTH:2 * B_KEY_WIDTH]
    vb = pb[:, 2 * B_KEY_WIDTH:2 * B_KEY_WIDTH + B_WIDTH]
    zb = pb[:, 2 * B_KEY_WIDTH + B_WIDTH:2 * B_KEY_WIDTH + 2 * B_WIDTH]
    alr = pb[:, 2 * B_KEY_WIDTH + 2 * B_WIDTH:]
    zg = _dot(alr, walpha_ref[...]) + balpha_ref[...]
    log_a = (jnp.minimum(zg, 0.0) - jnp.log(1.0 + jnp.exp(-jnp.abs(zg)))) / B_GATE_TEMP
    ob = _gla(qb, kb, vb, log_a, st_ref)
    normed = []
    for hh in range(B_HEADS):
        oh = ob[:, hh * B_VAL_DIM:(hh + 1) * B_VAL_DIM]
        normed.append(oh * lax.rsqrt(jnp.mean(oh * oh, axis=-1, keepdims=True) + NORM_EPS))
    ob = jnp.concatenate(normed, axis=1) * ggla_ref[...] * _silu(zb)
    yb = _dot(ob, wob_ref[...])

    pg = _dot(hb, wcat_ref[:, _COL_G:_COL_END])
    merged = _sigmoid(pg[:, :D_MODEL]) * ya + _sigmoid(pg[:, D_MODEL:]) * yb
    y = x + _dot(merged, wo_ref[...])
    out_ref[...] = (y * lax.rsqrt(jnp.mean(y * y, axis=-1, keepdims=True) + NORM_EPS)
                    * gfin_ref[...])


def _const_spec(shape):
    zeros = (0,) * len(shape)
    return pl.BlockSpec(shape, lambda b, s: zeros, pipeline_mode=pl.Buffered(1))


def _fused_layer(x, pos3, sinks, g_in, w_cat, w_alpha, b_alpha, g_gla, w_oa, w_ob, w_o, g_fin):
    bsz, seq, d = x.shape
    assert d == D_MODEL and seq % SEQ_TILE == 0 and SEQ_TILE % WINDOW == 0
    grid = (bsz, seq // SEQ_TILE)
    tile = pl.BlockSpec((None, SEQ_TILE, d), lambda b, s: (b, s, 0))
    in_specs = [
        pl.BlockSpec(memory_space=pltpu.SMEM),
        tile,
        pl.BlockSpec((None, 1, SEQ_TILE), lambda b, s: (b, 0, s)),
        _const_spec(g_in.shape), _const_spec(w_cat.shape), _const_spec(w_alpha.shape),
        _const_spec(b_alpha.shape), _const_spec(g_gla.shape), _const_spec(w_oa.shape),
        _const_spec(w_ob.shape), _const_spec(w_o.shape), _const_spec(g_fin.shape),
    ]
    return pl.pallas_call(
        _body,
        grid=grid,
        in_specs=in_specs,
        out_specs=tile,
        out_shape=jax.ShapeDtypeStruct(x.shape, x.dtype),
        scratch_shapes=[
            pltpu.VMEM((WINDOW, A_KV_WIDTH), jnp.float32),
            pltpu.VMEM((WINDOW, A_KV_WIDTH), jnp.float32),
            pltpu.VMEM((B_HEADS, B_VAL_DIM, LANES), jnp.float32),
        ],
        compiler_params=pltpu.CompilerParams(
            dimension_semantics=("arbitrary", "arbitrary"),
            vmem_limit_bytes=VMEM_LIMIT_BYTES),
        name="hybrid_swa_gla_layer",
    )(sinks, x, pos3, g_in, w_cat, w_alpha, b_alpha, g_gla, w_oa, w_ob, w_o, g_fin)


def kernel(x, positions, g_in, w_in, w_alpha_up, b_alpha, attn_sinks, g_gla_norm, w_out_a, w_out_b,
           w_o, g_final):
    assert g_in.shape[0] == 1, "single-layer configuration"
    w = w_in[0]
    n_ab = 2 * A_WIDTH + 2 * A_KV_WIDTH + 2 * B_KEY_WIDTH + 2 * B_WIDTH
    w_cat = jnp.concatenate(
        [w[:, :n_ab + B_GATE_RANK],
         jnp.zeros((D_MODEL, _ALR_PAD - B_GATE_RANK), w.dtype),
         w[:, n_ab + B_GATE_RANK:]], axis=1).astype(MXU_DTYPE)
    w_alpha = jnp.concatenate(
        [w_alpha_up[0], jnp.zeros((_ALR_PAD - B_GATE_RANK, B_KEY_WIDTH), w_alpha_up.dtype)],
        axis=0).astype(MXU_DTYPE)
    bsz, seq = positions.shape
    return _fused_layer(
        x, positions.reshape(bsz, 1, seq), attn_sinks[0], g_in, w_cat, w_alpha, b_alpha,
        g_gla_norm, w_out_a[0].astype(MXU_DTYPE), w_out_b[0].astype(MXU_DTYPE),
        w_o[0].astype(MXU_DTYPE), g_final.reshape(1, D_MODEL))
```

```python
import math

import jax
import jax.numpy as jnp
from jax import lax
from jax.experimental import pallas as pl
from jax.experimental.pallas import tpu as pltpu

D_MODEL = 1024
A_HEADS = 8
A_KV_HEADS = 2
A_HEAD_DIM = 64
A_WIDTH = A_HEADS * A_HEAD_DIM
A_KV_WIDTH = A_KV_HEADS * A_HEAD_DIM
WINDOW = 128
ROPE_THETA = 500000.0
ROPE_DIM = A_HEAD_DIM // 4
ROPE_HALF = ROPE_DIM // 2
B_HEADS = 4
B_KEY_DIM = 64
B_VAL_DIM = 128
B_KEY_WIDTH = B_HEADS * B_KEY_DIM
B_WIDTH = B_HEADS * B_VAL_DIM
B_GATE_RANK = 16
B_GATE_TEMP = 16.0
B_CHUNK = 64
NORM_EPS = 1e-6
NEG_BIG = -1e30

LANES = 128
VMEM_LIMIT_BYTES = 56 * 1024 * 1024

SEQ_TILE = 256
MXU_DTYPE = jnp.bfloat16

_COL_A = 0
_COL_B = _COL_A + 2 * A_WIDTH + 2 * A_KV_WIDTH
_ALR_PAD = LANES
_COL_G = _COL_B + 2 * B_KEY_WIDTH + 2 * B_WIDTH + _ALR_PAD
_COL_END = _COL_G + 2 * D_MODEL

_NT = (((1,), (1,)), ((), ()))
_TN = (((0,), (0,)), ((), ()))


def _dot(a, b, dims=None):
    a = a.astype(MXU_DTYPE)
    b = b.astype(MXU_DTYPE)
    if dims is None:
        return jnp.dot(a, b, preferred_element_type=jnp.float32)
    return lax.dot_general(a, b, dims, preferred_element_type=jnp.float32)


def _sigmoid(z):
    return 1.0 / (1.0 + jnp.exp(-z))


def _silu(z):
    return z * _sigmoid(z)


def _rope_tables(pos_row):
    t = pos_row.shape[1]
    fidx = lax.broadcasted_iota(jnp.int32, (ROPE_HALF, 1), 0).astype(jnp.float32)
    inv_freq = jnp.exp(-math.log(ROPE_THETA) * fidx * (2.0 / ROPE_DIM))
    ang = pos_row.astype(jnp.float32) * inv_freq
    stacked = jnp.concatenate(
        [jnp.cos(ang), jnp.sin(ang), jnp.zeros((LANES - 2 * ROPE_HALF, t), jnp.float32)], axis=0)
    tt = stacked.T
    lane = lax.broadcasted_iota(jnp.int32, (t, LANES), 1)
    lo = lane < A_HEAD_DIM
    l64 = lane & (A_HEAD_DIM - 1)
    first = l64 < ROPE_HALF
    second = jnp.logical_and(l64 >= ROPE_HALF, l64 < ROPE_DIM)
    r8 = pltpu.roll(tt, ROPE_HALF, 1)
    r64 = pltpu.roll(tt, A_HEAD_DIM, 1)
    r72 = pltpu.roll(tt, A_HEAD_DIM + ROPE_HALF, 1)
    r120 = pltpu.roll(tt, LANES - ROPE_HALF, 1)
    r56 = pltpu.roll(tt, A_HEAD_DIM - ROPE_HALF, 1)
    cos_t = jnp.where(first, jnp.where(lo, tt, r64), jnp.where(second, jnp.where(lo, r8, r72), 1.0))
    sin_a = jnp.where(first, -jnp.where(lo, r120, r56), 0.0)
    sin_b = jnp.where(second, jnp.where(lo, tt, r64), 0.0)
    return cos_t, sin_a, sin_b


def _rope(xg, tables):
    cos_t, sin_a, sin_b = tables
    return (xg * cos_t + pltpu.roll(xg, LANES - ROPE_HALF, 1) * sin_a
            + pltpu.roll(xg, ROPE_HALF, 1) * sin_b)


def _dup_halves(x):
    lane = lax.broadcasted_iota(jnp.int32, x.shape, 1)
    lo = lane < A_HEAD_DIM
    r = pltpu.roll(x, A_HEAD_DIM, 1)
    return jnp.where(lo, x, r), jnp.where(lo, r, x)


def _attention(q_rot, k_all, v_all, sinks_ref, first_key0, fillers=()):
    t = q_rot.shape[0]
    nblk = t // WINDOW
    grp = A_HEADS // A_KV_HEADS
    rows = grp * WINDOW
    kd = _dup_halves(k_all)
    vd = _dup_halves(v_all)
    kd = [a.astype(MXU_DTYPE) for a in kd]
    vd = [a.astype(MXU_DTYPE) for a in vd]

    lane = lax.broadcasted_iota(jnp.int32, (WINDOW, LANES), 1)
    lo = lane < A_HEAD_DIM
    r_i = lax.broadcasted_iota(jnp.int32, (rows, 2 * WINDOW), 0) & (WINDOW - 1)
    k_j = lax.broadcasted_iota(jnp.int32, (rows, 2 * WINDOW), 1) - WINDOW
    band = jnp.logical_and(k_j <= r_i, r_i - k_j < WINDOW)
    row1 = lax.broadcasted_iota(jnp.int32, (rows, 1), 0)

    out_blocks = []
    for i in range(nblk):
        qs = q_rot[i * WINDOW:(i + 1) * WINDOW, :]
        valid = jnp.logical_and(band, k_j >= first_key0) if i == 0 else band
        pair_out = []
        for g in range(A_KV_HEADS):
            parts = []
            for p in range(grp // 2):
                qp = qs[:, (g * (grp // 2) + p) * LANES:(g * (grp // 2) + p + 1) * LANES]
                parts.append(jnp.where(lo, qp, 0.0))
                parts.append(jnp.where(lo, 0.0, qp))
            q4 = jnp.concatenate(parts, axis=0)
            kb = kd[g][i * WINDOW:i * WINDOW + 2 * WINDOW, :]
            vb = vd[g][i * WINDOW:i * WINDOW + 2 * WINDOW, :]
            s = _dot(q4, kb, _NT)
            if fillers:
                fillers.pop(0)()
            s = jnp.where(valid, s, NEG_BIG)
            sink = sinks_ref[g * grp + grp - 1]
            sink_col = jnp.full((rows, 1), sink, jnp.float32)
            for j in range(grp - 2, -1, -1):
                sink_col = jnp.where(row1 < (j + 1) * WINDOW, sinks_ref[g * grp + j], sink_col)
            m = jnp.maximum(jnp.max(s, axis=-1, keepdims=True), sink_col)
            p_ = jnp.exp(s - m)
            denom = jnp.sum(p_, axis=-1, keepdims=True) + jnp.exp(sink_col - m)
            o4 = _dot(p_, vb) / denom
            for p in range(grp // 2):
                pair_out.append(jnp.where(lo, o4[(2 * p) * WINDOW:(2 * p + 1) * WINDOW, :],
                                          o4[(2 * p + 1) * WINDOW:(2 * p + 2) * WINDOW, :]))
        out_blocks.append(jnp.concatenate(pair_out, axis=1))
    return jnp.concatenate(out_blocks, axis=0)


def _chunk_cumsum(g):
    row = lax.broadcasted_iota(jnp.int32, g.shape, 0) & (B_CHUNK - 1)
    k = 1
    while k < B_CHUNK:
        g = g + jnp.where(row >= k, pltpu.roll(g, k, 0), 0.0)
        k *= 2
    return g


def _gla(qb, kb, vb, log_a, st_ref):
    t = qb.shape[0]
    nchunk = t // B_CHUNK
    cum_all = _chunk_cumsum(log_a)
    lane = lax.broadcasted_iota(jnp.int32, (B_CHUNK, LANES), 1)
    lo = lane < B_KEY_DIM
    ri = lax.broadcasted_iota(jnp.int32, (2 * B_CHUNK, B_CHUNK), 0) & (B_CHUNK - 1)
    ci = lax.broadcasted_iota(jnp.int32, (2 * B_CHUNK, B_CHUNK), 1)
    causal = ci <= ri
    scale = B_KEY_DIM ** -0.5

    states = [st_ref[h] for h in range(B_HEADS)]
    out_chunks = []
    for c in range(nchunk):
        r0, r1 = c * B_CHUNK, (c + 1) * B_CHUNK
        head_out = [None] * B_HEADS
        for p in range(B_HEADS // 2):
            l0, l1 = p * LANES, (p + 1) * LANES
            cum = cum_all[r0:r1, l0:l1]
            last = cum[B_CHUNK - 1:B_CHUNK, :]
            mid = cum[B_CHUNK // 2 - 1:B_CHUNK // 2, :]
            qs = qb[r0:r1, l0:l1] * scale
            ks = kb[r0:r1, l0:l1]
            qe = qs * jnp.exp(cum - mid)
            ke = ks * jnp.exp(mid - cum)
            qg = qs * jnp.exp(cum)
            kl = ks * jnp.exp(last - cum)
            decay = jnp.exp(last)
            qe2 = jnp.concatenate([jnp.where(lo, qe, 0.0), jnp.where(lo, 0.0, qe)], axis=0)
            a2 = _dot(qe2, ke, _NT)
            a2 = jnp.where(causal, a2, 0.0)
            for half in range(2):
                h = 2 * p + half
                sel = lo if half == 0 else jnp.logical_not(lo)
                vh = vb[r0:r1, h * B_VAL_DIM:(h + 1) * B_VAL_DIM]
                attn = a2[half * B_CHUNK:(half + 1) * B_CHUNK, :]
                o_intra = _dot(attn, vh)
                o_inter = _dot(jnp.where(sel, qg, 0.0), states[h], _NT)
                head_out[h] = o_intra + o_inter
                inc = _dot(vh, jnp.where(sel, kl, 0.0), _TN)
                states[h] = states[h] * decay + inc
        out_chunks.append(jnp.concatenate(head_out, axis=1))
    for h in range(B_HEADS):
        st_ref[h] = states[h]
    return jnp.concatenate(out_chunks, axis=0)


def _body(sinks_ref, x_ref, pos_ref, gin_ref, wcat_ref, walpha_ref, balpha_ref, ggla_ref,
          woa_ref, wob_ref, wo_ref, gfin_ref, out_ref, kprev_ref, vprev_ref, st_ref):
    s_idx = pl.program_id(1)

    @pl.when(s_idx == 0)
    def _():
        kprev_ref[...] = jnp.zeros_like(kprev_ref)
        vprev_ref[...] = jnp.zeros_like(vprev_ref)
        st_ref[...] = jnp.zeros_like(st_ref)

    x = x_ref[...]
    t = x.shape[0]
    h = x * lax.rsqrt(jnp.mean(x * x, axis=-1, keepdims=True) + NORM_EPS) * gin_ref[...]
    hb = h.astype(MXU_DTYPE)

    pa = _dot(hb, wcat_ref[:, _COL_A:_COL_B])
    tables = _rope_tables(pos_ref[...])
    qscale = A_HEAD_DIM ** -0.5
    q_rot = jnp.concatenate(
        [_rope(pa[:, j * LANES:(j + 1) * LANES], tables) * qscale for j in range(A_WIDTH // LANES)],
        axis=1)
    k_rot = _rope(pa[:, A_WIDTH:A_WIDTH + A_KV_WIDTH], tables)
    va = pa[:, A_WIDTH + A_KV_WIDTH:A_WIDTH + 2 * A_KV_WIDTH]
    za = pa[:, A_WIDTH + 2 * A_KV_WIDTH:2 * A_WIDTH + 2 * A_KV_WIDTH]
    k_all = jnp.concatenate([kprev_ref[...], k_rot], axis=0)
    v_all = jnp.concatenate([vprev_ref[...], va], axis=0)
    kprev_ref[...] = k_rot[t - WINDOW:, :]
    vprev_ref[...] = va[t - WINDOW:, :]
    first_key0 = jnp.where(s_idx > 0, -WINDOW, 0)
    n_units = (t // WINDOW) * A_KV_HEADS
    bounds = [_COL_B + (_COL_END - _COL_B) * u // n_units // LANES * LANES for u in range(n_units)]
    bounds.append(_COL_END)
    proj_chunks = []

    def _make_filler(c0, c1):
        return lambda: proj_chunks.append(_dot(hb, wcat_ref[:, c0:c1]))

    fillers = [_make_filler(bounds[u], bounds[u + 1]) for u in range(n_units)]
    oa = _attention(q_rot, k_all, v_all, sinks_ref, first_key0, fillers) * _silu(za)
    ya = _dot(oa, woa_ref[...])
    pbg = jnp.concatenate(proj_chunks, axis=1)

    pb = pbg[:, :_COL_G - _COL_B]
    qb = pb[:, 0:B_KEY_WIDTH]
    kb = pb[:, B_KEY_WIDTH:2 * B_KEY_WIDTH]
    vb = pb[:, 2 * B_KEY_WIDTH:2 * B_KEY_WIDTH + B_WIDTH]
    zb = pb[:, 2 * B_KEY_WIDTH + B_WIDTH:2 * B_KEY_WIDTH + 2 * B_WIDTH]
    alr = pb[:, 2 * B_KEY_WIDTH + 2 * B_WIDTH:]
    zg = _dot(alr, walpha_ref[...]) + balpha_ref[...]
    log_a = (jnp.minimum(zg, 0.0) - jnp.log(1.0 + jnp.exp(-jnp.abs(zg)))) / B_GATE_TEMP
    ob = _gla(qb, kb, vb, log_a, st_ref)
    normed = []
    for hh in range(B_HEADS):
        oh = ob[:, hh * B_VAL_DIM:(hh + 1) * B_VAL_DIM]
        normed.append(oh * lax.rsqrt(jnp.mean(oh * oh, axis=-1, keepdims=True) + NORM_EPS))
    ob = jnp.concatenate(normed, axis=1) * ggla_ref[...] * _silu(zb)
    yb = _dot(ob, wob_ref[...])

    pg = pbg[:, _COL_G - _COL_B:]
    merged = _sigmoid(pg[:, :D_MODEL]) * ya + _sigmoid(pg[:, D_MODEL:]) * yb
    y = x + _dot(merged, wo_ref[...])
    out_ref[...] = (y * lax.rsqrt(jnp.mean(y * y, axis=-1, keepdims=True) + NORM_EPS)
                    * gfin_ref[...])


def _const_spec(shape):
    zeros = (0,) * len(shape)
    return pl.BlockSpec(shape, lambda b, s: zeros, pipeline_mode=pl.Buffered(1))


def _fused_layer(x, pos3, sinks, g_in, w_cat, w_alpha, b_alpha, g_gla, w_oa, w_ob, w_o, g_fin):
    bsz, seq, d = x.shape
    assert d == D_MODEL and seq % SEQ_TILE == 0 and SEQ_TILE % WINDOW == 0
    grid = (bsz, seq // SEQ_TILE)
    tile = pl.BlockSpec((None, SEQ_TILE, d), lambda b, s: (b, s, 0))
    in_specs = [
        pl.BlockSpec(memory_space=pltpu.SMEM),
        tile,
        pl.BlockSpec((None, 1, SEQ_TILE), lambda b, s: (b, 0, s)),
        _const_spec(g_in.shape), _const_spec(w_cat.shape), _const_spec(w_alpha.shape),
        _const_spec(b_alpha.shape), _const_spec(g_gla.shape), _const_spec(w_oa.shape),
        _const_spec(w_ob.shape), _const_spec(w_o.shape), _const_spec(g_fin.shape),
    ]
    return pl.pallas_call(
        _body,
        grid=grid,
        in_specs=in_specs,
        out_specs=tile,
        out_shape=jax.ShapeDtypeStruct(x.shape, x.dtype),
        scratch_shapes=[
            pltpu.VMEM((WINDOW, A_KV_WIDTH), jnp.float32),
            pltpu.VMEM((WINDOW, A_KV_WIDTH), jnp.float32),
            pltpu.VMEM((B_HEADS, B_VAL_DIM, LANES), jnp.float32),
        ],
        compiler_params=pltpu.CompilerParams(
            dimension_semantics=("arbitrary", "arbitrary"),
            vmem_limit_bytes=VMEM_LIMIT_BYTES),
        name="hybrid_swa_gla_layer",
    )(sinks, x, pos3, g_in, w_cat, w_alpha, b_alpha, g_gla, w_oa, w_ob, w_o, g_fin)


def kernel(x, positions, g_in, w_in, w_alpha_up, b_alpha, attn_sinks, g_gla_norm, w_out_a, w_out_b,
           w_o, g_final):
    assert g_in.shape[0] == 1, "single-layer configuration"
    w = w_in[0]
    n_ab = 2 * A_WIDTH + 2 * A_KV_WIDTH + 2 * B_KEY_WIDTH + 2 * B_WIDTH
    w_cat = jnp.concatenate(
        [w[:, :n_ab + B_GATE_RANK],
         jnp.zeros((D_MODEL, _ALR_PAD - B_GATE_RANK), w.dtype),
         w[:, n_ab + B_GATE_RANK:]], axis=1).astype(MXU_DTYPE)
    w_alpha = jnp.concatenate(
        [w_alpha_up[0], jnp.zeros((_ALR_PAD - B_GATE_RANK, B_KEY_WIDTH), w_alpha_up.dtype)],
        axis=0).astype(MXU_DTYPE)
    bsz, seq = positions.shape
    return _fused_layer(
        x, positions.reshape(bsz, 1, seq), attn_sinks[0], g_in, w_cat, w_alpha, b_alpha,
        g_gla_norm, w_out_a[0].astype(MXU_DTYPE), w_out_b[0].astype(MXU_DTYPE),
        w_o[0].astype(MXU_DTYPE), g_final.reshape(1, D_MODEL))
```

```python
import math

import jax
import jax.numpy as jnp
from jax import lax
from jax.experimental import pallas as pl
from jax.experimental.pallas import tpu as pltpu

D_MODEL = 1024
A_HEADS = 8
A_KV_HEADS = 2
A_HEAD_DIM = 64
A_WIDTH = A_HEADS * A_HEAD_DIM
A_KV_WIDTH = A_KV_HEADS * A_HEAD_DIM
WINDOW = 128
ROPE_THETA = 500000.0
ROPE_DIM = A_HEAD_DIM // 4
ROPE_HALF = ROPE_DIM // 2
B_HEADS = 4
B_KEY_DIM = 64
B_VAL_DIM = 128
B_KEY_WIDTH = B_HEADS * B_KEY_DIM
B_WIDTH = B_HEADS * B_VAL_DIM
B_GATE_RANK = 16
B_GATE_TEMP = 16.0
B_CHUNK = 64
NORM_EPS = 1e-6
NEG_BIG = -1e30

LANES = 128
MXU_COLS = 256
VMEM_LIMIT_BYTES = 56 * 1024 * 1024

SEQ_TILE = 256
MXU_DTYPE = jnp.bfloat16

_ALR_PAD = LANES
_C_QA = 0
_C_KA = _C_QA + A_WIDTH
_C_VA = _C_KA + A_KV_WIDTH
_C_ZA = _C_VA + A_KV_WIDTH
_C_QB = _C_ZA + A_WIDTH
_C_KB = _C_QB + B_KEY_WIDTH
_C_VB = _C_KB + B_KEY_WIDTH
_C_ZB = _C_VB + B_WIDTH
_C_ALR = _C_ZB + B_WIDTH
_C_GA = _C_ALR + _ALR_PAD
_C_GB = _C_GA + D_MODEL
_C_END = _C_GB + D_MODEL

_FILL_ROPE = 2
_FILL_ATTN_UNIT = 3
_FILL_YA = 1
_FILL_GLA_CHUNK = 1

_NT = (((1,), (1,)), ((), ()))
_TN = (((0,), (0,)), ((), ()))


def _dot(a, b, dims=None):
    a = a.astype(MXU_DTYPE)
    b = b.astype(MXU_DTYPE)
    if dims is None:
        return jnp.dot(a, b, preferred_element_type=jnp.float32)
    return lax.dot_general(a, b, dims, preferred_element_type=jnp.float32)


def _sigmoid(z):
    return 1.0 / (1.0 + jnp.exp(-z))


def _silu(z):
    return z * _sigmoid(z)


def _rms(x, g):
    return x * lax.rsqrt(jnp.mean(x * x, axis=-1, keepdims=True) + NORM_EPS) * g


class _Filler:
    def __init__(self, thunks):
        self._thunks = list(thunks)

    def __call__(self, n):
        for _ in range(min(n, len(self._thunks))):
            self._thunks.pop(0)()

    def drain(self):
        self(len(self._thunks))


def _rope_tables(pos_row):
    t = pos_row.shape[1]
    fidx = lax.broadcasted_iota(jnp.int32, (ROPE_HALF, 1), 0).astype(jnp.float32)
    inv_freq = jnp.exp(-math.log(ROPE_THETA) * fidx * (2.0 / ROPE_DIM))
    ang = pos_row.astype(jnp.float32) * inv_freq
    stacked = jnp.concatenate(
        [jnp.cos(ang), jnp.sin(ang), jnp.zeros((LANES - 2 * ROPE_HALF, t), jnp.float32)], axis=0)
    tt = stacked.T
    lane = lax.broadcasted_iota(jnp.int32, (t, LANES), 1)
    lo = lane < A_HEAD_DIM
    l64 = lane & (A_HEAD_DIM - 1)
    first = l64 < ROPE_HALF
    second = jnp.logical_and(l64 >= ROPE_HALF, l64 < ROPE_DIM)
    r8 = pltpu.roll(tt, ROPE_HALF, 1)
    r64 = pltpu.roll(tt, A_HEAD_DIM, 1)
    r72 = pltpu.roll(tt, A_HEAD_DIM + ROPE_HALF, 1)
    r120 = pltpu.roll(tt, LANES - ROPE_HALF, 1)
    r56 = pltpu.roll(tt, A_HEAD_DIM - ROPE_HALF, 1)
    cos_t = jnp.where(first, jnp.where(lo, tt, r64), jnp.where(second, jnp.where(lo, r8, r72), 1.0))
    sin_a = jnp.where(first, -jnp.where(lo, r120, r56), 0.0)
    sin_b = jnp.where(second, jnp.where(lo, tt, r64), 0.0)
    return cos_t, sin_a, sin_b


def _rope(xg, tables):
    cos_t, sin_a, sin_b = tables
    return (xg * cos_t + pltpu.roll(xg, LANES - ROPE_HALF, 1) * sin_a
            + pltpu.roll(xg, ROPE_HALF, 1) * sin_b)


def _dup_halves(x):
    lane = lax.broadcasted_iota(jnp.int32, x.shape, 1)
    lo = lane < A_HEAD_DIM
    r = pltpu.roll(x, A_HEAD_DIM, 1)
    return jnp.where(lo, x, r), jnp.where(lo, r, x)


def _attention(q_rot, k_all, v_all, sinks_ref, first_key0, fill):
    t = q_rot.shape[0]
    nblk = t // WINDOW
    grp = A_HEADS // A_KV_HEADS
    rows = grp * WINDOW
    kd = _dup_halves(k_all)
    vd = _dup_halves(v_all)
    kd = [a.astype(MXU_DTYPE) for a in kd]
    vd = [a.astype(MXU_DTYPE) for a in vd]

    lane = lax.broadcasted_iota(jnp.int32, (WINDOW, LANES), 1)
    lo = lane < A_HEAD_DIM
    r_i = lax.broadcasted_iota(jnp.int32, (rows, 2 * WINDOW), 0) & (WINDOW - 1)
    k_j = lax.broadcasted_iota(jnp.int32, (rows, 2 * WINDOW), 1) - WINDOW
    band = jnp.logical_and(k_j <= r_i, r_i - k_j < WINDOW)
    row1 = lax.broadcasted_iota(jnp.int32, (rows, 1), 0)

    out_blocks = []
    for i in range(nblk):
        qs = q_rot[i * WINDOW:(i + 1) * WINDOW, :]
        valid = jnp.logical_and(band, k_j >= first_key0) if i == 0 else band
        pair_out = []
        for g in range(A_KV_HEADS):
            parts = []
            for p in range(grp // 2):
                qp = qs[:, (g * (grp // 2) + p) * LANES:(g * (grp // 2) + p + 1) * LANES]
                parts.append(jnp.where(lo, qp, 0.0))
                parts.append(jnp.where(lo, 0.0, qp))
            q4 = jnp.concatenate(parts, axis=0)
            kb = kd[g][i * WINDOW:i * WINDOW + 2 * WINDOW, :]
            vb = vd[g][i * WINDOW:i * WINDOW + 2 * WINDOW, :]
            s = _dot(q4, kb, _NT)
            fill(_FILL_ATTN_UNIT)
            s = jnp.where(valid, s, NEG_BIG)
            sink = sinks_ref[g * grp + grp - 1]
            sink_col = jnp.full((rows, 1), sink, jnp.float32)
            for j in range(grp - 2, -1, -1):
                sink_col = jnp.where(row1 < (j + 1) * WINDOW, sinks_ref[g * grp + j], sink_col)
            m = jnp.maximum(jnp.max(s, axis=-1, keepdims=True), sink_col)
            p_ = jnp.exp(s - m)
            denom = jnp.sum(p_, axis=-1, keepdims=True) + jnp.exp(sink_col - m)
            o4 = _dot(p_, vb) / denom
            for p in range(grp // 2):
                pair_out.append(jnp.where(lo, o4[(2 * p) * WINDOW:(2 * p + 1) * WINDOW, :],
                                          o4[(2 * p + 1) * WINDOW:(2 * p + 2) * WINDOW, :]))
        out_blocks.append(jnp.concatenate(pair_out, axis=1))
    return jnp.concatenate(out_blocks, axis=0)


def _chunk_cumsum(g):
    row = lax.broadcasted_iota(jnp.int32, g.shape, 0) & (B_CHUNK - 1)
    k = 1
    while k < B_CHUNK:
        g = g + jnp.where(row >= k, pltpu.roll(g, k, 0), 0.0)
        k *= 2
    return g


def _gla(qb, kb, vb, log_a, st_ref, fill):
    t = qb.shape[0]
    nchunk = t // B_CHUNK
    cum_all = _chunk_cumsum(log_a)
    lane = lax.broadcasted_iota(jnp.int32, (B_CHUNK, LANES), 1)
    lo = lane < B_KEY_DIM
    ri = lax.broadcasted_iota(jnp.int32, (2 * B_CHUNK, B_CHUNK), 0) & (B_CHUNK - 1)
    ci = lax.broadcasted_iota(jnp.int32, (2 * B_CHUNK, B_CHUNK), 1)
    causal = ci <= ri
    scale = B_KEY_DIM ** -0.5

    states = [st_ref[h] for h in range(B_HEADS)]
    out_chunks = []
    for c in range(nchunk):
        r0, r1 = c * B_CHUNK, (c + 1) * B_CHUNK
        head_out = [None] * B_HEADS
        for p in range(B_HEADS // 2):
            l0, l1 = p * LANES, (p + 1) * LANES
            cum = cum_all[r0:r1, l0:l1]
            last = cum[B_CHUNK - 1:B_CHUNK, :]
            mid = cum[B_CHUNK // 2 - 1:B_CHUNK // 2, :]
            qs = qb[r0:r1, l0:l1] * scale
            ks = kb[r0:r1, l0:l1]
            qe = qs * jnp.exp(cum - mid)
            ke = ks * jnp.exp(mid - cum)
            qg = qs * jnp.exp(cum)
            kl = ks * jnp.exp(last - cum)
            decay = jnp.exp(last)
            qe2 = jnp.concatenate([jnp.where(lo, qe, 0.0), jnp.where(lo, 0.0, qe)], axis=0)
            a2 = _dot(qe2, ke, _NT)
            a2 = jnp.where(causal, a2, 0.0)
            for half in range(2):
                h = 2 * p + half
                sel = lo if half == 0 else jnp.logical_not(lo)
                vh = vb[r0:r1, h * B_VAL_DIM:(h + 1) * B_VAL_DIM]
                attn = a2[half * B_CHUNK:(half + 1) * B_CHUNK, :]
                o_intra = _dot(attn, vh)
                o_inter = _dot(jnp.where(sel, qg, 0.0), states[h], _NT)
                head_out[h] = o_intra + o_inter
                inc = _dot(vh, jnp.where(sel, kl, 0.0), _TN)
                states[h] = states[h] * decay + inc
        fill(_FILL_GLA_CHUNK)
        out_chunks.append(jnp.concatenate(head_out, axis=1))
    for h in range(B_HEADS):
        st_ref[h] = states[h]
    return jnp.concatenate(out_chunks, axis=0)


def _back(proj, x, pos_row, first_key0, fill, before_out_proj, sinks_ref, walpha_ref,
          balpha_ref, ggla_ref, woa_ref, wob_ref, wo_ref, gfin_ref, kprev_ref, vprev_ref, st_ref):
    t = x.shape[0]

    tables = _rope_tables(pos_row)
    qscale = A_HEAD_DIM ** -0.5
    q_rot = jnp.concatenate(
        [_rope(proj(_C_QA + j * LANES, _C_QA + (j + 1) * LANES), tables) * qscale
         for j in range(A_WIDTH // LANES)], axis=1)
    k_rot = _rope(proj(_C_KA, _C_VA), tables)
    va = proj(_C_VA, _C_ZA)
    fill(_FILL_ROPE)
    k_all = jnp.concatenate([kprev_ref[...], k_rot], axis=0)
    v_all = jnp.concatenate([vprev_ref[...], va], axis=0)
    kprev_ref[...] = k_rot[t - WINDOW:, :]
    vprev_ref[...] = va[t - WINDOW:, :]
    oa = _attention(q_rot, k_all, v_all, sinks_ref, first_key0, fill) * _silu(proj(_C_ZA, _C_QB))
    ya = _dot(oa, woa_ref[...])
    fill(_FILL_YA)

    zg = _dot(proj(_C_ALR, _C_GA), walpha_ref[...]) + balpha_ref[...]
    log_a = (jnp.minimum(zg, 0.0) - jnp.log(1.0 + jnp.exp(-jnp.abs(zg)))) / B_GATE_TEMP
    ob = _gla(proj(_C_QB, _C_KB), proj(_C_KB, _C_VB), proj(_C_VB, _C_ZB), log_a, st_ref, fill)
    normed = []
    for hh in range(B_HEADS):
        oh = ob[:, hh * B_VAL_DIM:(hh + 1) * B_VAL_DIM]
        normed.append(oh * lax.rsqrt(jnp.mean(oh * oh, axis=-1, keepdims=True) + NORM_EPS))
    ob = jnp.concatenate(normed, axis=1) * ggla_ref[...] * _silu(proj(_C_ZB, _C_ALR))
    yb = _dot(ob, wob_ref[...])

    merged = _sigmoid(proj(_C_GA, _C_GB)) * ya + _sigmoid(proj(_C_GB, _C_END)) * yb
    before_out_proj()
    y = x + _dot(merged, wo_ref[...])
    fill.drain()
    return _rms(y, gfin_ref[...])


def _chunk_bounds():
    return [(c, min(c + MXU_COLS, _C_END)) for c in range(0, _C_END, MXU_COLS)]


def _body(half_seq, sinks_ref, xn_ref, xb_ref, pos_ref, gin_ref, wcat_ref, walpha_ref, balpha_ref,
          ggla_ref, woa_ref, wob_ref, wo_ref, gfin_ref, out_ref, kprev_ref, vprev_ref, st_ref,
          hb_ref, proj_ref):
    j = pl.program_id(0)
    t = SEQ_TILE
    seq_start = lax.rem(j - 1, half_seq) == 0

    @pl.when(j == 0)
    def _():
        hb_ref[...] = jnp.zeros_like(hb_ref)
        proj_ref[...] = jnp.zeros_like(proj_ref)

    @pl.when(jnp.logical_or(j == 0, seq_start))
    def _():
        kprev_ref[...] = jnp.zeros_like(kprev_ref)
        vprev_ref[...] = jnp.zeros_like(vprev_ref)
        st_ref[...] = jnp.zeros_like(st_ref)

    shared = (sinks_ref, walpha_ref, balpha_ref, ggla_ref, woa_ref, wob_ref, wo_ref, gfin_ref,
              kprev_ref, vprev_ref, st_ref)
    bounds = _chunk_bounds()

    hb_odd = hb_ref[...]
    odd_chunks = []
    fill = _Filler([
        (lambda c0=c0, c1=c1: odd_chunks.append(_dot(hb_odd, wcat_ref[:, c0:c1])))
        for c0, c1 in bounds])
    hb_even = []
    out_ref[0:t, :] = _back(
        lambda c0, c1: proj_ref[:, c0:c1], xb_ref[0:t, :], pos_ref[:, 0:t],
        jnp.where(seq_start, 0, -WINDOW), fill,
        lambda: hb_even.append(_rms(xn_ref[0:t, :], gin_ref[...]).astype(MXU_DTYPE)), *shared)

    proj_odd = jnp.concatenate(odd_chunks, axis=1)

    def _store_chunk(c0, c1):
        proj_ref[:, c0:c1] = _dot(hb_even[0], wcat_ref[:, c0:c1])

    def _store_hb():
        hb_ref[...] = _rms(xn_ref[t:2 * t, :], gin_ref[...]).astype(MXU_DTYPE)

    fill = _Filler([(lambda c0=c0, c1=c1: _store_chunk(c0, c1)) for c0, c1 in bounds])
    out_ref[t:2 * t, :] = _back(
        lambda c0, c1: proj_odd[:, c0:c1], xb_ref[t:2 * t, :], pos_ref[:, t:2 * t],
        -WINDOW, fill, _store_hb, *shared)


def _const_spec(shape):
    zeros = (0,) * len(shape)
    return pl.BlockSpec(shape, lambda j: zeros, pipeline_mode=pl.Buffered(1))


def _fused_layer(x2, pos3, seq, sinks, g_in, w_cat, w_alpha, b_alpha, g_gla, w_oa, w_ob, w_o,
                 g_fin):
    tokens, d = x2.shape
    pair = 2 * SEQ_TILE
    assert d == D_MODEL and SEQ_TILE % WINDOW == 0 and seq % pair == 0 and tokens % seq == 0
    npairs = tokens // pair
    nchunks = len(_chunk_bounds())
    assert (_FILL_ROPE + _FILL_ATTN_UNIT * (SEQ_TILE // WINDOW) * A_KV_HEADS + _FILL_YA
            + _FILL_GLA_CHUNK * (SEQ_TILE // B_CHUNK)) <= nchunks
    lagged = lambda j: (jnp.maximum(j - 1, 0), 0)
    in_specs = [
        pl.BlockSpec(memory_space=pltpu.SMEM),
        pl.BlockSpec((pair, d), lambda j: (jnp.minimum(j, npairs - 1), 0)),
        pl.BlockSpec((pair, d), lagged),
        pl.BlockSpec((None, 1, pair), lambda j: (jnp.maximum(j - 1, 0), 0, 0)),
        _const_spec(g_in.shape), _const_spec(w_cat.shape), _const_spec(w_alpha.shape),
        _const_spec(b_alpha.shape), _const_spec(g_gla.shape), _const_spec(w_oa.shape),
        _const_spec(w_ob.shape), _const_spec(w_o.shape), _const_spec(g_fin.shape),
    ]
    body = lambda *refs: _body(seq // pair, *refs)
    return pl.pallas_call(
        body,
        grid=(npairs + 1,),
        in_specs=in_specs,
        out_specs=pl.BlockSpec((pair, d), lagged),
        out_shape=jax.ShapeDtypeStruct(x2.shape, x2.dtype),
        scratch_shapes=[
            pltpu.VMEM((WINDOW, A_KV_WIDTH), jnp.float32),
            pltpu.VMEM((WINDOW, A_KV_WIDTH), jnp.float32),
            pltpu.VMEM((B_HEADS, B_VAL_DIM, LANES), jnp.float32),
            pltpu.VMEM((SEQ_TILE, D_MODEL), MXU_DTYPE),
            pltpu.VMEM((SEQ_TILE, _C_END), jnp.float32),
        ],
        compiler_params=pltpu.CompilerParams(
            dimension_semantics=("arbitrary",),
            vmem_limit_bytes=VMEM_LIMIT_BYTES),
        name="hybrid_swa_gla_layer",
    )(sinks, x2, x2, pos3, g_in, w_cat, w_alpha, b_alpha, g_gla, w_oa, w_ob, w_o, g_fin)


def kernel(x, positions, g_in, w_in, w_alpha_up, b_alpha, attn_sinks, g_gla_norm, w_out_a, w_out_b,
           w_o, g_final):
    assert g_in.shape[0] == 1, "single-layer configuration"
    w = w_in[0]
    n_ab = _C_ALR
    w_cat = jnp.concatenate(
        [w[:, :n_ab + B_GATE_RANK],
         jnp.zeros((D_MODEL, _ALR_PAD - B_GATE_RANK), w.dtype),
         w[:, n_ab + B_GATE_RANK:]], axis=1).astype(MXU_DTYPE)
    w_alpha = jnp.concatenate(
        [w_alpha_up[0], jnp.zeros((_ALR_PAD - B_GATE_RANK, B_KEY_WIDTH), w_alpha_up.dtype)],
        axis=0).astype(MXU_DTYPE)
    bsz, seq, d = x.shape
    pair = 2 * SEQ_TILE
    out = _fused_layer(
        x.reshape(bsz * seq, d), positions.reshape(bsz * seq // pair, 1, pair), seq,
        attn_sinks[0], g_in, w_cat, w_alpha, b_alpha, g_gla_norm, w_out_a[0].astype(MXU_DTYPE),
        w_out_b[0].astype(MXU_DTYPE), w_o[0].astype(MXU_DTYPE), g_final.reshape(1, D_MODEL))
    return out.reshape(bsz, seq, d)
```
